```python
import math
import jax, jax.numpy as jnp
from jax import lax
import numpy as np

D_MODEL = 1024
BATCH = 32
SEQ = 256
DEPTH = 4
DEC_BATCH = 2
DEC_SEQ = 1024
PAST_LEN = 512

GRID_W = 64
MIX_WIDTH = D_MODEL
A_WIDTH = MIX_WIDTH // 2
A_HEADS = 4
A_V_DIM = A_WIDTH // A_HEADS
A_QK_DIM = A_V_DIM // 2
B_WIDTH = MIX_WIDTH // 4
C_WIDTH = MIX_WIDTH - A_WIDTH - B_WIDTH
C_GROUPS = 4
C_GROUP_DIM = C_WIDTH // C_GROUPS
CHUNK = 128
CONV_W = 3
D_FF = 4 * D_MODEL
Q_BLOCK = 128
ROPE_BASE = 10000.0
ROPE_AXIS_DIM = A_QK_DIM // 2
IN_WIDTH = 3 * A_WIDTH + 3 * B_WIDTH + 2 * C_WIDTH
EPS = 1e-6

kernel_name = 'hybrid_diffattn_conv_chunkmlp_prefix_dit'


def _rmsnorm(x, g):
    xf = x.astype(jnp.float32)
    y = xf * lax.rsqrt(jnp.mean(xf * xf, axis=-1, keepdims=True) + EPS)
    return (y * g.astype(jnp.float32)).astype(x.dtype)


def _rope_tables(n_tokens):
    rows = n_tokens // GRID_W
    row = jnp.repeat(jnp.arange(rows, dtype=jnp.float32), GRID_W)
    col = jnp.tile(jnp.arange(GRID_W, dtype=jnp.float32), rows)
    inv = ROPE_BASE ** (-jnp.arange(0, ROPE_AXIS_DIM, 2, dtype=jnp.float32) / ROPE_AXIS_DIM)
    ang_r = row[:, None] * inv
    ang_c = col[:, None] * inv
    return (jnp.cos(ang_r), jnp.sin(ang_r), jnp.cos(ang_c), jnp.sin(ang_c))


def _rot_half(x, cos, sin):
    x1, x2 = jnp.split(x, 2, axis=-1)
    return jnp.concatenate([x1 * cos - x2 * sin, x2 * cos + x1 * sin], axis=-1)


def _apply_axial_rope(x, tables):
    cr, sr, cc, sc = tables
    b, t, h, _ = x.shape
    xf = x.astype(jnp.float32).reshape(b, t, h, 2, A_QK_DIM)
    bc = lambda a: a[:, None, None, :]
    xr, xc = jnp.split(xf, 2, axis=-1)
    out = jnp.concatenate([_rot_half(xr, bc(cr), bc(sr)), _rot_half(xc, bc(cc), bc(sc))], axis=-1)
    return out.reshape(b, t, h, 2 * A_QK_DIM).astype(x.dtype)


def _project(h, w_in_l):
    p = h @ w_in_l
    sizes = (A_WIDTH,) * 3 + (B_WIDTH,) * 3 + (C_WIDTH,) * 2
    idx = [int(s) for s in np.cumsum(sizes)[:-1]]
    q, k, v, gb, gc, xin, u, vs = jnp.split(p, idx, axis=-1)
    b, t, _ = h.shape
    q = q.reshape(b, t, A_HEADS, 2 * A_QK_DIM)
    k = k.reshape(b, t, A_HEADS, 2 * A_QK_DIM)
    v = v.reshape(b, t, A_HEADS, A_V_DIM)
    return q, k, v, gb, gc, xin, u, vs


def _diff_attention(q, k, v, lam, subln_g, lam_init):
    b, tq = q.shape[0], q.shape[1]
    tk = k.shape[1]
    n_blk = tq // Q_BLOCK
    k2 = k.reshape(b, tk, A_HEADS, 2, A_QK_DIM)
    qb = q.reshape(b, n_blk, Q_BLOCK, A_HEADS, 2, A_QK_DIM).transpose(1, 0, 2, 3, 4, 5)
    scale = A_QK_DIM ** -0.5

    def one_block(qblk):
        s = jnp.einsum('bqhcd,bkhcd->bchqk', qblk, k2, preferred_element_type=jnp.float32) * scale
        p = jax.nn.softmax(s, axis=-1)
        a = p[:, 0] - lam * p[:, 1]
        return jnp.einsum('bhqk,bkhd->bqhd', a.astype(v.dtype), v)

    o = lax.map(one_block, qb)
    o = o.transpose(1, 0, 2, 3, 4).reshape(b, tq, A_HEADS, A_V_DIM)
    o = _rmsnorm(o, subln_g) * (1.0 - lam_init)
    return o.reshape(b, tq, A_WIDTH)


def _short_conv(x, w, bias):
    xp = jnp.pad(x, ((0, 0), (1, 1), (0, 0)))
    return xp[:, :-2] * w[0] + xp[:, 1:-1] * w[1] + xp[:, 2:] * w[2] + bias


def _chunk_mlp(u, vs, w_s, b_s):
    b, t, _ = vs.shape
    vc = vs.reshape(b, t // CHUNK, CHUNK, C_GROUPS, C_GROUP_DIM)
    mixed = jnp.einsum('gpq,bnqgd->bnpgd', w_s, vc) + b_s.T[None, None, :, :, None]
    return u * mixed.reshape(b, t, C_WIDTH)


def _lambda(lq1, lk1, lq2, lk2, lam_init):
    f = lambda a: a.astype(jnp.float32)
    return jnp.exp(jnp.sum(f(lq1) * f(lk1))) - jnp.exp(jnp.sum(f(lq2) * f(lk2))) + lam_init


def _mlp(h, w_up_l, w_down_l):
    a = jax.nn.relu(h @ w_up_l)
    return (a * a) @ w_down_l


def setup_inputs(seed: int = 0) -> dict:
    key = jax.random.key(seed)
    ks = jax.random.split(key, 24)
    n = lambda k, s, sc: jax.random.normal(k, s, jnp.float32) * sc
    return {
        'x_prompt': n(ks[0], (BATCH, SEQ, D_MODEL), 1.0),
        'x_sample': n(ks[1], (DEC_BATCH, DEC_SEQ, D_MODEL), 1.0),
        'cache_k': n(ks[2], (DEC_BATCH, DEPTH, PAST_LEN, A_HEADS, 2 * A_QK_DIM), 1.0),
        'cache_v': n(ks[3], (DEC_BATCH, DEPTH, PAST_LEN, A_HEADS, A_V_DIM), 1.0),
        'c': n(ks[4], (DEC_BATCH, D_MODEL), 1.0),
        'c_ctx': n(ks[5], (D_MODEL,), 1.0),
        'w_mod': n(ks[6], (DEPTH, D_MODEL, 6 * D_MODEL), 0.5 * D_MODEL ** -0.5),
        'b_mod': n(ks[7], (DEPTH, 6 * D_MODEL), 0.02),
        'norm_mix': 1.0 + n(ks[8], (DEPTH, D_MODEL), 0.02),
        'norm_mlp': 1.0 + n(ks[9], (DEPTH, D_MODEL), 0.02),
        'w_in': n(ks[10], (DEPTH, D_MODEL, IN_WIDTH), D_MODEL ** -0.5),
        'lam_q1': n(ks[11], (DEPTH, A_QK_DIM), 0.1),
        'lam_k1': n(ks[12], (DEPTH, A_QK_DIM), 0.1),
        'lam_q2': n(ks[13], (DEPTH, A_QK_DIM), 0.1),
        'lam_k2': n(ks[14], (DEPTH, A_QK_DIM), 0.1),
        'subln': 1.0 + n(ks[15], (DEPTH, A_V_DIM), 0.02),
        'conv_w': n(ks[16], (DEPTH, CONV_W, B_WIDTH), CONV_W ** -0.5),
        'conv_b': n(ks[17], (DEPTH, B_WIDTH), 0.02),
        'w_s': n(ks[18], (DEPTH, C_GROUPS, CHUNK, CHUNK), CHUNK ** -0.5),
        'b_s': 1.0 + n(ks[19], (DEPTH, C_GROUPS, CHUNK), 0.02),
        'w_out': n(ks[20], (DEPTH, MIX_WIDTH, D_MODEL), MIX_WIDTH ** -0.5),
        'w_up': n(ks[21], (DEPTH, D_MODEL, D_FF), D_MODEL ** -0.5),
        'w_down': n(ks[22], (DEPTH, D_FF, D_MODEL), D_FF ** -0.5),
        'norm_final': 1.0 + n(ks[23], (D_MODEL,), 0.02),
    }


def reference(x_prompt, x_sample, cache_k, cache_v, c, c_ctx, w_mod, b_mod, norm_mix, norm_mlp,
              w_in, lam_q1, lam_k1, lam_q2, lam_k2, subln, conv_w, conv_b, w_s, b_s,
              w_out, w_up, w_down, norm_final):
    xp = x_prompt
    xs = x_sample
    rope = _rope_tables(xs.shape[1])
    sc_ctx = jax.nn.silu(c_ctx)
    sc_lat = jax.nn.silu(c)
    new_k, new_v = [], []
    for d in range(DEPTH):
        lam_init = 0.8 - 0.6 * math.exp(-0.3 * d)
        lam = _lambda(lam_q1[d], lam_k1[d], lam_q2[d], lam_k2[d], lam_init)

        mod_p = sc_ctx @ w_mod[d] + b_mod[d]
        sa, ca, ga, sm, cm, gm = jnp.split(mod_p, 6, axis=-1)
        h = _rmsnorm(xp, norm_mix[d]) * (1.0 + ca) + sa
        q, k, v, gb, gc, xin, u, vs = _project(h, w_in[d])
        att = _diff_attention(q, k, v, lam, subln[d], lam_init)
        conv = gb * _short_conv(gc * xin, conv_w[d], conv_b[d])
        cmlp = _chunk_mlp(u, vs, w_s[d], b_s[d])
        xp = xp + ga * (jnp.concatenate([att, conv, cmlp], axis=-1) @ w_out[d])
        h = _rmsnorm(xp, norm_mlp[d]) * (1.0 + cm) + sm
        xp = xp + gm * _mlp(h, w_up[d], w_down[d])
        new_k.append(k)
        new_v.append(v)

        mod_s = (sc_lat @ w_mod[d] + b_mod[d])[:, None, :]
        sa, ca, ga, sm, cm, gm = jnp.split(mod_s, 6, axis=-1)
        h = _rmsnorm(xs, norm_mix[d]) * (1.0 + ca) + sa
        q, k, v, gb, gc, xin, u, vs = _project(h, w_in[d])
        q = _apply_axial_rope(q, rope)
        k = _apply_axial_rope(k, rope)
        k_all = jnp.concatenate([cache_k[:, d].astype(k.dtype), k], axis=1)
        v_all = jnp.concatenate([cache_v[:, d].astype(v.dtype), v], axis=1)
        att = _diff_attention(q, k_all, v_all, lam, subln[d], lam_init)
        conv = gb * _short_conv(gc * xin, conv_w[d], conv_b[d])
        cmlp = _chunk_mlp(u, vs, w_s[d], b_s[d])
        xs = xs + ga * (jnp.concatenate([att, conv, cmlp], axis=-1) @ w_out[d])
        h = _rmsnorm(xs, norm_mlp[d]) * (1.0 + cm) + sm
        xs = xs + gm * _mlp(h, w_up[d], w_down[d])

    y_prompt = _rmsnorm(xp, norm_final)
    y_sample = _rmsnorm(xs, norm_final)
    new_cache_k = jnp.stack(new_k, axis=1)
    new_cache_v = jnp.stack(new_v, axis=1)
    return (y_prompt, y_sample, new_cache_k, new_cache_v)
```

```python
import functools
import math

import numpy as np
import jax
import jax.numpy as jnp
from jax import lax
from jax.experimental import pallas as pl
from jax.experimental.pallas import tpu as pltpu

D_MODEL = 1024
DEPTH = 4
GRID_W = 64
A_WIDTH = 512
A_HEADS = 4
A_V_DIM = 128
A_QK_DIM = 64
B_WIDTH = 256
C_WIDTH = 256
C_GROUPS = 4
C_GROUP_DIM = 64
CHUNK = 128
D_FF = 4 * D_MODEL
ROPE_BASE = 10000.0
ROPE_AXIS_DIM = A_QK_DIM // 2
IN_WIDTH = 3 * A_WIDTH + 3 * B_WIDTH + 2 * C_WIDTH
EPS = 1e-6

OFF_Q, OFF_K, OFF_V = 0, A_WIDTH, 2 * A_WIDTH
OFF_GB = 3 * A_WIDTH
OFF_GC = OFF_GB + B_WIDTH
OFF_XIN = OFF_GC + B_WIDTH
OFF_U = OFF_XIN + B_WIDTH
OFF_VS = OFF_U + C_WIDTH

ROW_TILE = 256
FF_CHUNK = 1024
MOD_COLS = 1536
MOD_ROWS = 8
CONV_PAD = 8
VMEM_LIMIT_BYTES = 56 * 1024 * 1024

BF16 = jnp.bfloat16
F32 = jnp.float32


def _dot(a, b):
    return jnp.dot(a, b, preferred_element_type=F32)


def _dot_nt(a, b):
    return lax.dot_general(a, b, (((1,), (1,)), ((), ())), preferred_element_type=F32)


def _rmsnorm(x, g):
    ms = jnp.mean(x * x, axis=-1, keepdims=True)
    return (x * lax.rsqrt(ms + EPS)) * g


def _mod_parts(mod_row):
    return [mod_row[:, i * D_MODEL:(i + 1) * D_MODEL] for i in range(6)]


def _lambda(lamv, lam_init):
    s1 = jnp.sum(lamv[0:1] * lamv[1:2], axis=-1, keepdims=True)
    s2 = jnp.sum(lamv[2:3] * lamv[3:4], axis=-1, keepdims=True)
    return jnp.exp(s1) - jnp.exp(s2) + lam_init


def _diff_attention(q, load_k, load_v, lam, subln_g, one_minus_lam_init):
    lane = lax.broadcasted_iota(jnp.int32, (1, A_V_DIM), 1)
    first = lane < A_QK_DIM
    outs = []
    for h in range(A_HEADS):
        qh = q[:, h * A_V_DIM:(h + 1) * A_V_DIM]
        kh = load_k(h)
        probs = []
        for comp_mask in (first, jnp.logical_not(first)):
            qc = jnp.where(comp_mask, qh, 0.0).astype(BF16)
            s = _dot_nt(qc, kh)
            e = jnp.exp(s - jnp.max(s, axis=-1, keepdims=True))
            probs.append((e, 1.0 / jnp.sum(e, axis=-1, keepdims=True)))
        (e0, inv0), (e1, inv1) = probs
        a = e0 * inv0 - e1 * (lam * inv1)
        o = _dot(a.astype(BF16), load_v(h))
        outs.append(_rmsnorm(o, subln_g) * one_minus_lam_init)
    return jnp.concatenate(outs, axis=-1)


def _chunk_mlp(u, vs, ws_all, b_full):
    lane = lax.broadcasted_iota(jnp.int32, (1, C_WIDTH), 1)
    rows = []
    for n in range(vs.shape[0] // CHUNK):
        vc = vs[n * CHUNK:(n + 1) * CHUNK].astype(BF16)
        r = _dot(ws_all, vc)
        mixed = r[(C_GROUPS - 1) * CHUNK:]
        for g in range(C_GROUPS - 2, -1, -1):
            mixed = jnp.where(lane < (g + 1) * C_GROUP_DIM, r[g * CHUNK:(g + 1) * CHUNK], mixed)
        rows.append(mixed + b_full)
    return u * jnp.concatenate(rows, axis=0)


def _conv_taps(zp, z, zn, gb, conv_w, conv_b):
    return gb * (zp * conv_w[0:1] + z * conv_w[1:2] + zn * conv_w[2:3] + conv_b)


def _out_and_mlp(x, mix_ref, mod, norm_mlp, w_out_ref, w_up_ref, w_down_ref):
    _, _, ga, sm, cm, gm = mod
    x1 = x + ga * _dot(mix_ref[...], w_out_ref[0])
    h2 = (_rmsnorm(x1, norm_mlp) * (1.0 + cm) + sm).astype(BF16)
    acc = jnp.zeros_like(x1)
    for j in range(D_FF // FF_CHUNK):
        a = jnp.maximum(_dot(h2, w_up_ref[0, :, j * FF_CHUNK:(j + 1) * FF_CHUNK]), 0.0)
        acc = acc + _dot((a * a).astype(BF16), w_down_ref[0, j * FF_CHUNK:(j + 1) * FF_CHUNK, :])
    return x1 + gm * acc


def _mod_kernel(c_ref, w_ref, b_ref, o_ref):
    c = c_ref[...]
    s = (c * jax.nn.sigmoid(c)).astype(BF16)
    o_ref[0] = _dot(s, w_ref[0].astype(BF16)) + b_ref[0]


def _mod_call(c_rows, w_mod, b_mod):
    n_col = 6 * D_MODEL // MOD_COLS
    return pl.pallas_call(
        _mod_kernel,
        grid=(DEPTH, n_col),
        in_specs=[
            pl.BlockSpec((MOD_ROWS, D_MODEL), lambda d, j: (0, 0)),
            pl.BlockSpec((1, D_MODEL, MOD_COLS), lambda d, j: (d, 0, j)),
            pl.BlockSpec((1, 1, MOD_COLS), lambda d, j: (d, 0, j)),
        ],
        out_specs=pl.BlockSpec((1, MOD_ROWS, MOD_COLS), lambda d, j: (d, 0, j)),
        out_shape=jax.ShapeDtypeStruct((DEPTH, MOD_ROWS, 6 * D_MODEL), F32),
        compiler_params=pltpu.CompilerParams(
            dimension_semantics=("arbitrary", "arbitrary"), vmem_limit_bytes=VMEM_LIMIT_BYTES),
        name="mod",
    )(c_rows, w_mod, b_mod.reshape(DEPTH, 1, 6 * D_MODEL))


def _ctx_kernel(x_ref, mod_ref, nmix_ref, nmlp_ref, w_in_ref, w_out_ref, w_up_ref, w_down_ref,
                lamv_ref, lamc_ref, subln_ref, cw_ref, cb_ref, ws_ref, bfull_ref, nfin_ref,
                y_ref, k_ref, v_ref, mix_ref):
    d = pl.program_id(0)
    seq = x_ref.shape[1]
    x = x_ref[0]
    mod = _mod_parts(mod_ref[0, 0])
    sa, ca = mod[0], mod[1]
    lam_init = lamc_ref[0, :, 0:1]
    one_minus_lam_init = lamc_ref[0, :, 1:2]
    lam = _lambda(lamv_ref[0], lam_init)

    h = (_rmsnorm(x, nmix_ref[0]) * (1.0 + ca) + sa).astype(BF16)
    qkv = _dot(h, w_in_ref[0, :, OFF_Q:OFF_GB])
    k = qkv[:, OFF_K:OFF_V]
    v = qkv[:, OFF_V:OFF_GB]
    k_ref[0, 0] = k
    v_ref[0, 0] = v
    k_bf = k.astype(BF16)
    v_bf = v.astype(BF16)
    q = qkv[:, OFF_Q:OFF_K] * (A_QK_DIM ** -0.5)
    att = _diff_attention(
        q,
        lambda hh: k_bf[:, hh * A_V_DIM:(hh + 1) * A_V_DIM],
        lambda hh: v_bf[:, hh * A_V_DIM:(hh + 1) * A_V_DIM],
        lam, subln_ref[0], one_minus_lam_init)
    mix_ref[:, 0:A_WIDTH] = att.astype(BF16)

    rest = _dot(h, w_in_ref[0, :, OFF_GB:IN_WIDTH])
    r0 = OFF_GB
    gb = rest[:, OFF_GB - r0:OFF_GC - r0]
    z = rest[:, OFF_GC - r0:OFF_XIN - r0] * rest[:, OFF_XIN - r0:OFF_U - r0]
    row = lax.broadcasted_iota(jnp.int32, (seq, 1), 0)
    zp = jnp.where(row == 0, 0.0, pltpu.roll(z, 1, 0))
    zn = jnp.where(row == seq - 1, 0.0, pltpu.roll(z, seq - 1, 0))
    conv = _conv_taps(zp, z, zn, gb, cw_ref[0], cb_ref[0])
    mix_ref[:, A_WIDTH:A_WIDTH + B_WIDTH] = conv.astype(BF16)

    cmlp = _chunk_mlp(rest[:, OFF_U - r0:OFF_VS - r0], rest[:, OFF_VS - r0:], ws_ref[0], bfull_ref[0])
    mix_ref[:, A_WIDTH + B_WIDTH:] = cmlp.astype(BF16)

    x2 = _out_and_mlp(x, mix_ref, mod, nmlp_ref[0], w_out_ref, w_up_ref, w_down_ref)

    @pl.when(d < DEPTH - 1)
    def _():
        y_ref[0] = x2

    @pl.when(d == DEPTH - 1)
    def _():
        y_ref[0] = _rmsnorm(x2, nfin_ref[...])


def _layer_spec(shape, n_grid, **kw):
    zeros = (0,) * (len(shape) - 1)
    if n_grid == 2:
        imap = lambda d, b: (d,) + zeros
    else:
        imap = lambda d, b, p, t: (d,) + zeros
    return pl.BlockSpec((1,) + tuple(shape[1:]), imap, **kw)


def _ctx_call(x, mod4, p):
    batch, seq, _ = x.shape
    assert seq == ROW_TILE
    single = dict(pipeline_mode=pl.Buffered(1))
    in_specs = [
        pl.BlockSpec((1, seq, D_MODEL), lambda d, b: (b, 0, 0)),
        pl.BlockSpec((1, 1, 1, 6 * D_MODEL), lambda d, b: (d, 0, 0, 0)),
        _layer_spec(p["norm_mix"].shape, 2),
        _layer_spec(p["norm_mlp"].shape, 2),
        _layer_spec(p["w_in"].shape, 2, **single),
        _layer_spec(p["w_out"].shape, 2, **single),
        _layer_spec(p["w_up"].shape, 2, **single),
        _layer_spec(p["w_down"].shape, 2, **single),
        _layer_spec(p["lamv"].shape, 2),
        _layer_spec(p["lamc"].shape, 2),
        _layer_spec(p["subln"].shape, 2),
        _layer_spec(p["conv_w"].shape, 2),
        _layer_spec(p["conv_b"].shape, 2),
        _layer_spec(p["ws_all"].shape, 2),
        _layer_spec(p["b_full"].shape, 2),
        pl.BlockSpec((1, D_MODEL), lambda d, b: (0, 0)),
    ]
    out_specs = [
        pl.BlockSpec((1, seq, D_MODEL), lambda d, b: (b, 0, 0)),
        pl.BlockSpec((1, 1, seq, A_WIDTH), lambda d, b: (b, d, 0, 0)),
        pl.BlockSpec((1, 1, seq, A_WIDTH), lambda d, b: (b, d, 0, 0)),
    ]
    out_shape = [
        jax.ShapeDtypeStruct(x.shape, F32),
        jax.ShapeDtypeStruct((batch, DEPTH, seq, A_WIDTH), F32),
        jax.ShapeDtypeStruct((batch, DEPTH, seq, A_WIDTH), F32),
    ]
    return pl.pallas_call(
        _ctx_kernel,
        grid=(DEPTH, batch),
        in_specs=in_specs,
        out_specs=out_specs,
        out_shape=out_shape,
        scratch_shapes=[pltpu.VMEM((seq, D_MODEL), BF16)],
        input_output_aliases={0: 0},
        compiler_params=pltpu.CompilerParams(
            dimension_semantics=("arbitrary", "arbitrary"), vmem_limit_bytes=VMEM_LIMIT_BYTES),
        name="ctx",
    )(x, mod4, p["norm_mix"], p["norm_mlp"], p["w_in"], p["w_out"], p["w_up"], p["w_down"],
      p["lamv"], p["lamc"], p["subln"], p["conv_w"], p["conv_b"], p["ws_all"], p["b_full"],
      p["norm_final"])


def _rope(xh, cos, sin_signed):
    lane = lax.broadcasted_iota(jnp.int32, (1, A_V_DIM), 1)
    half = ROPE_AXIS_DIM // 2
    low = (lane & (ROPE_AXIS_DIM - 1)) < half
    partner = jnp.where(low, pltpu.roll(xh, A_V_DIM - half, 1), pltpu.roll(xh, half, 1))
    return xh * cos + partner * sin_signed


def _lat_kernel(x_ref, ck_ref, cv_ref, mod_ref, cos_ref, sin_ref, nmix_ref, nmlp_ref,
                w_in_ref, w_out_ref, w_up_ref, w_down_ref, lamv_ref, lamc_ref, subln_ref,
                cw_ref, cb_ref, ws_ref, bfull_ref, nfin_ref,
                y_ref, q_s, k_s, v_s, gb_s, z_s, cm_s, mix_ref):
    d = pl.program_id(0)
    ph = pl.program_id(2)
    t = pl.program_id(3)
    past = ck_ref.shape[2]
    n_tok = q_s.shape[0]
    t0 = pl.multiple_of(t * ROW_TILE, ROW_TILE)
    mod = _mod_parts(mod_ref[0, 0])

    @pl.when(ph == 0)
    def _project():
        @pl.when(t == 0)
        def _():
            k_s[0:past, :] = ck_ref[0, 0].astype(BF16)
            v_s[0:past, :] = cv_ref[0, 0].astype(BF16)
            z_s[0:CONV_PAD, :] = jnp.zeros((CONV_PAD, B_WIDTH), F32)
            z_s[CONV_PAD + n_tok:, :] = jnp.zeros((CONV_PAD, B_WIDTH), F32)

        sa, ca = mod[0], mod[1]
        h = (_rmsnorm(x_ref[0], nmix_ref[0]) * (1.0 + ca) + sa).astype(BF16)
        qkv = _dot(h, w_in_ref[0, :, OFF_Q:OFF_GB])
        cos = cos_ref[...]
        sin = sin_ref[...]
        for hh in range(A_HEADS):
            lo = hh * A_V_DIM
            qh = _rope(qkv[:, OFF_Q + lo:OFF_Q + lo + A_V_DIM], cos, sin) * (A_QK_DIM ** -0.5)
            kh = _rope(qkv[:, OFF_K + lo:OFF_K + lo + A_V_DIM], cos, sin)
            q_s[pl.ds(t0, ROW_TILE), lo:lo + A_V_DIM] = qh
            k_s[pl.ds(past + t0, ROW_TILE), lo:lo + A_V_DIM] = kh.astype(BF16)
        v_s[pl.ds(past + t0, ROW_TILE), :] = qkv[:, OFF_V:OFF_GB].astype(BF16)

        rest = _dot(h, w_in_ref[0, :, OFF_GB:IN_WIDTH])
        r0 = OFF_GB
        gb_s[pl.ds(t0, ROW_TILE), :] = rest[:, OFF_GB - r0:OFF_GC - r0]
        z_s[pl.ds(CONV_PAD + t0, ROW_TILE), :] = (
            rest[:, OFF_GC - r0:OFF_XIN - r0] * rest[:, OFF_XIN - r0:OFF_U - r0])
        cm_s[pl.ds(t0, ROW_TILE), :] = _chunk_mlp(
            rest[:, OFF_U - r0:OFF_VS - r0], rest[:, OFF_VS - r0:], ws_ref[0], bfull_ref[0])

    @pl.when(ph == 1)
    def _mix():
        lam_init = lamc_ref[0, :, 0:1]
        one_minus_lam_init = lamc_ref[0, :, 1:2]
        lam = _lambda(lamv_ref[0], lam_init)
        att = _diff_attention(
            q_s[pl.ds(t0, ROW_TILE), :],
            lambda hh: k_s[:, hh * A_V_DIM:(hh + 1) * A_V_DIM],
            lambda hh: v_s[:, hh * A_V_DIM:(hh + 1) * A_V_DIM],
            lam, subln_ref[0], one_minus_lam_init)
        mix_ref[:, 0:A_WIDTH] = att.astype(BF16)

        zwin = z_s[pl.ds(t0, ROW_TILE + 2 * CONV_PAD), :]
        zp = zwin[CONV_PAD - 1:CONV_PAD - 1 + ROW_TILE]
        z = zwin[CONV_PAD:CONV_PAD + ROW_TILE]
        zn = zwin[CONV_PAD + 1:CONV_PAD + 1 + ROW_TILE]
        conv = _conv_taps(zp, z, zn, gb_s[pl.ds(t0, ROW_TILE), :], cw_ref[0], cb_ref[0])
        mix_ref[:, A_WIDTH:A_WIDTH + B_WIDTH] = conv.astype(BF16)
        mix_ref[:, A_WIDTH + B_WIDTH:] = cm_s[pl.ds(t0, ROW_TILE), :].astype(BF16)

        x2 = _out_and_mlp(x_ref[0], mix_ref, mod, nmlp_ref[0], w_out_ref, w_up_ref, w_down_ref)

        @pl.when(d < DEPTH - 1)
        def _():
            y_ref[0] = x2

        @pl.when(d == DEPTH - 1)
        def _():
            y_ref[0] = _rmsnorm(x2, nfin_ref[...])


def _lat_call(x, cache_k, cache_v, mod4, cos, sin, p):
    batch, n_tok, _ = x.shape
    past = cache_k.shape[2]
    n_tiles = n_tok // ROW_TILE
    single = dict(pipeline_mode=pl.Buffered(1))
    in_specs = [
        pl.BlockSpec((1, ROW_TILE, D_MODEL), lambda d, b, ph, t: (b, t, 0)),
        pl.BlockSpec((1, 1, past, A_WIDTH), lambda d, b, ph, t: (b, d, 0, 0)),
        pl.BlockSpec((1, 1, past, A_WIDTH), lambda d, b, ph, t: (b, d, 0, 0)),
        pl.BlockSpec((1, 1, 1, 6 * D_MODEL), lambda d, b, ph, t: (d, 1 + b, 0, 0)),
        pl.BlockSpec((ROW_TILE, A_V_DIM), lambda d, b, ph, t: (t, 0)),
        pl.BlockSpec((ROW_TILE, A_V_DIM), lambda d, b, ph, t: (t, 0)),
        _layer_spec(p["norm_mix"].shape, 4),
        _layer_spec(p["norm_mlp"].shape, 4),
        _layer_spec(p["w_in"].shape, 4, **single),
        _layer_spec(p["w_out"].shape, 4, **single),
        _layer_spec(p["w_up"].shape, 4, **single),
        _layer_spec(p["w_down"].shape, 4, **single),
        _layer_spec(p["lamv"].shape, 4),
        _layer_spec(p["lamc"].shape, 4),
        _layer_spec(p["subln"].shape, 4),
        _layer_spec(p["conv_w"].shape, 4),
        _layer_spec(p["conv_b"].shape, 4),
        _layer_spec(p["ws_all"].shape, 4),
        _layer_spec(p["b_full"].shape, 4),
        pl.BlockSpec((1, D_MODEL), lambda d, b, ph, t: (0, 0)),
    ]
    out_specs = pl.BlockSpec((1, ROW_TILE, D_MODEL), lambda d, b, ph, t: (b, t * ph, 0))
    scratch = [
        pltpu.VMEM((n_tok, A_WIDTH), F32),
        pltpu.VMEM((past + n_tok, A_WIDTH), BF16),
        pltpu.VMEM((past + n_tok, A_WIDTH), BF16),
        pltpu.VMEM((n_tok, B_WIDTH), F32),
        pltpu.VMEM((n_tok + 2 * CONV_PAD, B_WIDTH), F32),
        pltpu.VMEM((n_tok, C_WIDTH), F32),
        pltpu.VMEM((ROW_TILE, D_MODEL), BF16),
    ]
    return pl.pallas_call(
        _lat_kernel,
        grid=(DEPTH, batch, 2, n_tiles),
        in_specs=in_specs,
        out_specs=out_specs,
        out_shape=jax.ShapeDtypeStruct(x.shape, F32),
        scratch_shapes=scratch,
        input_output_aliases={0: 0},
        compiler_params=pltpu.CompilerParams(
            dimension_semantics=("arbitrary",) * 4, vmem_limit_bytes=VMEM_LIMIT_BYTES),
        name="lat",
    )(x, cache_k, cache_v, mod4, cos, sin, p["norm_mix"], p["norm_mlp"], p["w_in"], p["w_out"],
      p["w_up"], p["w_down"], p["lamv"], p["lamc"], p["subln"], p["conv_w"], p["conv_b"],
      p["ws_all"], p["b_full"], p["norm_final"])


def _rope_tables(n_tokens):
    rows = n_tokens // GRID_W
    row = jnp.repeat(jnp.arange(rows, dtype=F32), GRID_W)
    col = jnp.tile(jnp.arange(GRID_W, dtype=F32), rows)
    inv = ROPE_BASE ** (-jnp.arange(0, ROPE_AXIS_DIM, 2, dtype=F32) / ROPE_AXIS_DIM)
    ang_r = row[:, None] * inv
    ang_c = col[:, None] * inv
    cos = jnp.concatenate([jnp.cos(ang_r)] * 2 + [jnp.cos(ang_c)] * 2, axis=-1)
    sin = jnp.concatenate([-jnp.sin(ang_r), jnp.sin(ang_r), -jnp.sin(ang_c), jnp.sin(ang_c)], axis=-1)
    reps = A_V_DIM // A_QK_DIM
    return jnp.tile(cos, (1, reps)), jnp.tile(sin, (1, reps))


def kernel(x_prompt, x_sample, cache_k, cache_v, c, c_ctx, w_mod, b_mod, norm_mix, norm_mlp, w_in,
           lam_q1, lam_k1, lam_q2, lam_k2, subln, conv_w, conv_b, w_s, b_s, w_out, w_up, w_down,
           norm_final):
    batch, seq, _ = x_prompt.shape
    dec_batch, dec_seq, _ = x_sample.shape
    past = cache_k.shape[2]
    assert 1 + dec_batch <= MOD_ROWS

    c_rows = jnp.zeros((MOD_ROWS, D_MODEL), F32).at[0].set(c_ctx).at[1:1 + dec_batch].set(c)
    mod = _mod_call(c_rows, w_mod, b_mod)
    mod4 = mod.reshape(DEPTH, MOD_ROWS, 1, 6 * D_MODEL)

    lam_inits = np.array([0.8 - 0.6 * math.exp(-0.3 * d) for d in range(DEPTH)], np.float64)
    lamc = np.zeros((DEPTH, 1, 128), np.float32)
    lamc[:, 0, 0] = lam_inits
    lamc[:, 0, 1] = 1.0 - lam_inits
    params = {
        "norm_mix": norm_mix.reshape(DEPTH, 1, D_MODEL),
        "norm_mlp": norm_mlp.reshape(DEPTH, 1, D_MODEL),
        "w_in": w_in.astype(BF16),
        "w_out": w_out.astype(BF16),
        "w_up": w_up.astype(BF16),
        "w_down": w_down.astype(BF16),
        "lamv": jnp.stack([lam_q1, lam_k1, lam_q2, lam_k2], axis=1),
        "lamc": jnp.asarray(lamc),
        "subln": subln.reshape(DEPTH, 1, A_V_DIM),
        "conv_w": conv_w,
        "conv_b": conv_b.reshape(DEPTH, 1, B_WIDTH),
        "ws_all": w_s.reshape(DEPTH, C_GROUPS * CHUNK, CHUNK).astype(BF16),
        "b_full": jnp.repeat(jnp.swapaxes(b_s, 1, 2), C_GROUP_DIM, axis=2),
        "norm_final": norm_final.reshape(1, D_MODEL),
    }

    y_prompt, new_k, new_v = _ctx_call(x_prompt, mod4, params)

    cos, sin = _rope_tables(dec_seq)
    y_sample = _lat_call(
        x_sample,
        cache_k.reshape(dec_batch, DEPTH, past, A_WIDTH),
        cache_v.reshape(dec_batch, DEPTH, past, A_WIDTH),
        mod4, cos, sin, params)

    return (y_prompt, y_sample,
            new_k.reshape(batch, DEPTH, seq, A_HEADS, 2 * A_QK_DIM),
            new_v.reshape(batch, DEPTH, seq, A_HEADS, A_V_DIM))
```

```python
import math

import numpy as np
import jax
import jax.numpy as jnp
from jax import lax
from jax.experimental import pallas as pl
from jax.experimental.pallas import tpu as pltpu

D_MODEL = 1024
DEPTH = 4
GRID_W = 64
A_WIDTH = 512
A_HEADS = 4
A_V_DIM = 128
A_QK_DIM = 64
B_WIDTH = 256
C_WIDTH = 256
C_GROUPS = 4
C_GROUP_DIM = 64
CHUNK = 128
D_FF = 4 * D_MODEL
ROPE_BASE = 10000.0
ROPE_AXIS_DIM = A_QK_DIM // 2
IN_WIDTH = 3 * A_WIDTH + 3 * B_WIDTH + 2 * C_WIDTH
EPS = 1e-6

OFF_Q, OFF_K, OFF_V = 0, A_WIDTH, 2 * A_WIDTH
OFF_GB = 3 * A_WIDTH
OFF_GC = OFF_GB + B_WIDTH
OFF_XIN = OFF_GC + B_WIDTH
OFF_U = OFF_XIN + B_WIDTH
OFF_VS = OFF_U + C_WIDTH

ROW_TILE = 256
FF_CHUNK = 1024
MOD_COLS = 1536
MOD_ROWS = 8
CONV_PAD = 8
VMEM_LIMIT_BYTES = 56 * 1024 * 1024

BF16 = jnp.bfloat16
F32 = jnp.float32


def _dot(a, b):
    return jnp.dot(a, b, preferred_element_type=F32)


def _dot_nt(a, b):
    return lax.dot_general(a, b, (((1,), (1,)), ((), ())), preferred_element_type=F32)


def _rmsnorm(x, g):
    ms = jnp.mean(x * x, axis=-1, keepdims=True)
    return (x * lax.rsqrt(ms + EPS)) * g


def _mod_parts(mod_row):
    return [mod_row[:, i * D_MODEL:(i + 1) * D_MODEL] for i in range(6)]


def _lambda(lamv, lam_init):
    s1 = jnp.sum(lamv[0:1] * lamv[1:2], axis=-1, keepdims=True)
    s2 = jnp.sum(lamv[2:3] * lamv[3:4], axis=-1, keepdims=True)
    return jnp.exp(s1) - jnp.exp(s2) + lam_init


def _diff_attention(q, load_k, load_v, lam, subln_g, one_minus_lam_init):
    lane = lax.broadcasted_iota(jnp.int32, (1, A_V_DIM), 1)
    first = lane < A_QK_DIM
    outs = []
    for h in range(A_HEADS):
        qh = q[:, h * A_V_DIM:(h + 1) * A_V_DIM]
        kh = load_k(h)
        probs = []
        for comp_mask in (first, jnp.logical_not(first)):
            qc = jnp.where(comp_mask, qh, 0.0).astype(BF16)
            s = _dot_nt(qc, kh)
            e = jnp.exp(s - jnp.max(s, axis=-1, keepdims=True))
            probs.append((e, 1.0 / jnp.sum(e, axis=-1, keepdims=True)))
        (e0, inv0), (e1, inv1) = probs
        a = e0 * inv0 - e1 * (lam * inv1)
        o = _dot(a.astype(BF16), load_v(h))
        outs.append(_rmsnorm(o, subln_g) * one_minus_lam_init)
    return jnp.concatenate(outs, axis=-1)


def _chunk_mlp(u, vs, ws_all, b_full):
    lane = lax.broadcasted_iota(jnp.int32, (1, C_WIDTH), 1)
    rows = []
    for n in range(vs.shape[0] // CHUNK):
        vc = vs[n * CHUNK:(n + 1) * CHUNK].astype(BF16)
        r = _dot(ws_all, vc)
        mixed = r[(C_GROUPS - 1) * CHUNK:]
        for g in range(C_GROUPS - 2, -1, -1):
            mixed = jnp.where(lane < (g + 1) * C_GROUP_DIM, r[g * CHUNK:(g + 1) * CHUNK], mixed)
        rows.append(mixed + b_full)
    return u * jnp.concatenate(rows, axis=0)


def _conv_taps(zp, z, zn, gb, conv_w, conv_b):
    return gb * (zp * conv_w[0:1] + z * conv_w[1:2] + zn * conv_w[2:3] + conv_b)


def _out_and_mlp(x, mix_ref, mod, norm_mlp, w_out_ref, w_up_ref, w_down_ref):
    _, _, ga, sm, cm, gm = mod
    x1 = x + ga * _dot(mix_ref[...], w_out_ref[0])
    h2 = (_rmsnorm(x1, norm_mlp) * (1.0 + cm) + sm).astype(BF16)
    acc = jnp.zeros_like(x1)
    for j in range(D_FF // FF_CHUNK):
        a = jnp.maximum(_dot(h2, w_up_ref[0, :, j * FF_CHUNK:(j + 1) * FF_CHUNK]), 0.0)
        acc = acc + _dot((a * a).astype(BF16), w_down_ref[0, j * FF_CHUNK:(j + 1) * FF_CHUNK, :])
    return x1 + gm * acc


def _mod_kernel(c_ref, w_ref, b_ref, o_ref):
    c = c_ref[...]
    s = (c * jax.nn.sigmoid(c)).astype(BF16)
    o_ref[0] = _dot(s, w_ref[0].astype(BF16)) + b_ref[0]


def _mod_call(c_rows, w_mod, b_mod):
    n_col = 6 * D_MODEL // MOD_COLS
    return pl.pallas_call(
        _mod_kernel,
        grid=(DEPTH, n_col),
        in_specs=[
            pl.BlockSpec((MOD_ROWS, D_MODEL), lambda d, j: (0, 0)),
            pl.BlockSpec((1, D_MODEL, MOD_COLS), lambda d, j: (d, 0, j)),
            pl.BlockSpec((1, 1, MOD_COLS), lambda d, j: (d, 0, j)),
        ],
        out_specs=pl.BlockSpec((1, MOD_ROWS, MOD_COLS), lambda d, j: (d, 0, j)),
        out_shape=jax.ShapeDtypeStruct((DEPTH, MOD_ROWS, 6 * D_MODEL), F32),
        compiler_params=pltpu.CompilerParams(
            dimension_semantics=("arbitrary", "arbitrary"), vmem_limit_bytes=VMEM_LIMIT_BYTES),
        name="mod",
    )(c_rows, w_mod, b_mod.reshape(DEPTH, 1, 6 * D_MODEL))


def _ctx_kernel(x_hbm, mod_ref, nmix_ref, nmlp_ref, w_in_ref, w_out_ref, w_up_ref, w_down_ref,
                lamv_ref, lamc_ref, subln_ref, cw_ref, cb_ref, ws_ref, bfull_ref, nfin_ref,
                y_hbm, k_ref, v_ref, xbuf, obuf, in_sem, out_sem, mix_ref):
    d = pl.program_id(0)
    b = pl.program_id(1)
    n_seq = pl.num_programs(1)
    seq = xbuf.shape[1]
    step = d * n_seq + b
    last_step = DEPTH * n_seq - 1
    slot = lax.rem(step, 2)

    def fetch(src_hbm, blk, sl):
        return pltpu.make_async_copy(src_hbm.at[blk], xbuf.at[sl], in_sem.at[sl])

    def store(blk, sl):
        return pltpu.make_async_copy(obuf.at[sl], y_hbm.at[blk], out_sem.at[sl])

    @pl.when(step == 0)
    def _():
        fetch(x_hbm, 0, 0).start()

    @pl.when(d == 0)
    def _():
        fetch(x_hbm, b, slot).wait()

        @pl.when(b + 1 < n_seq)
        def _():
            fetch(x_hbm, b + 1, 1 - slot).start()

    @pl.when(d > 0)
    def _():
        fetch(y_hbm, b, slot).wait()

    @pl.when(jnp.logical_and(step < last_step, jnp.logical_or(d > 0, b + 1 == n_seq)))
    def _():
        fetch(y_hbm, lax.rem(b + 1, n_seq), 1 - slot).start()

    x = xbuf[slot]
    mod = _mod_parts(mod_ref[0, 0])
    sa, ca = mod[0], mod[1]
    lam_init = lamc_ref[0, :, 0:1]
    one_minus_lam_init = lamc_ref[0, :, 1:2]
    lam = _lambda(lamv_ref[0], lam_init)

    h = (_rmsnorm(x, nmix_ref[0]) * (1.0 + ca) + sa).astype(BF16)
    qkv = _dot(h, w_in_ref[0, :, OFF_Q:OFF_GB])
    k = qkv[:, OFF_K:OFF_V]
    v = qkv[:, OFF_V:OFF_GB]
    for hh in range(A_HEADS):
        rows = pl.ds(hh, seq, stride=A_HEADS)
        k_ref[0, 0, rows, :] = k[:, hh * A_V_DIM:(hh + 1) * A_V_DIM]
        v_ref[0, 0, rows, :] = v[:, hh * A_V_DIM:(hh + 1) * A_V_DIM]
    k_bf = k.astype(BF16)
    v_bf = v.astype(BF16)
    q = qkv[:, OFF_Q:OFF_K] * (A_QK_DIM ** -0.5)
    att = _diff_attention(
        q,
        lambda hh: k_bf[:, hh * A_V_DIM:(hh + 1) * A_V_DIM],
        lambda hh: v_bf[:, hh * A_V_DIM:(hh + 1) * A_V_DIM],
        lam, subln_ref[0], one_minus_lam_init)
    mix_ref[:, 0:A_WIDTH] = att.astype(BF16)

    rest = _dot(h, w_in_ref[0, :, OFF_GB:IN_WIDTH])
    r0 = OFF_GB
    gb = rest[:, OFF_GB - r0:OFF_GC - r0]
    z = rest[:, OFF_GC - r0:OFF_XIN - r0] * rest[:, OFF_XIN - r0:OFF_U - r0]
    row = lax.broadcasted_iota(jnp.int32, (seq, 1), 0)
    zp = jnp.where(row == 0, 0.0, pltpu.roll(z, 1, 0))
    zn = jnp.where(row == seq - 1, 0.0, pltpu.roll(z, seq - 1, 0))
    conv = _conv_taps(zp, z, zn, gb, cw_ref[0], cb_ref[0])
    mix_ref[:, A_WIDTH:A_WIDTH + B_WIDTH] = conv.astype(BF16)

    cmlp = _chunk_mlp(rest[:, OFF_U - r0:OFF_VS - r0], rest[:, OFF_VS - r0:], ws_ref[0], bfull_ref[0])
    mix_ref[:, A_WIDTH + B_WIDTH:] = cmlp.astype(BF16)

    x2 = _out_and_mlp(x, mix_ref, mod, nmlp_ref[0], w_out_ref, w_up_ref, w_down_ref)

    @pl.when(step >= 2)
    def _():
        store(b, slot).wait()

    @pl.when(d < DEPTH - 1)
    def _():
        obuf[slot] = x2

    @pl.when(d == DEPTH - 1)
    def _():
        obuf[slot] = _rmsnorm(x2, nfin_ref[...])

    store(b, slot).start()

    @pl.when(step == last_step)
    def _():
        store(b, 1 - slot).wait()
        store(b, slot).wait()


def _layer_spec(shape, n_grid, **kw):
    zeros = (0,) * (len(shape) - 1)
    if n_grid == 2:
        imap = lambda d, b: (d,) + zeros
    else:
        imap = lambda d, b, p, t: (d,) + zeros
    return pl.BlockSpec((1,) + tuple(shape[1:]), imap, **kw)


def _ctx_call(x, mod4, p):
    batch, seq, _ = x.shape
    assert seq == ROW_TILE
    single = dict(pipeline_mode=pl.Buffered(1))
    in_specs = [
        pl.BlockSpec(memory_space=pl.ANY),
        pl.BlockSpec((1, 1, 1, 6 * D_MODEL), lambda d, b: (d, 0, 0, 0)),
        _layer_spec(p["norm_mix"].shape, 2),
        _layer_spec(p["norm_mlp"].shape, 2),
        _layer_spec(p["w_in"].shape, 2, **single),
        _layer_spec(p["w_out"].shape, 2, **single),
        _layer_spec(p["w_up"].shape, 2, **single),
        _layer_spec(p["w_down"].shape, 2, **single),
        _layer_spec(p["lamv"].shape, 2),
        _layer_spec(p["lamc"].shape, 2),
        _layer_spec(p["subln"].shape, 2),
        _layer_spec(p["conv_w"].shape, 2),
        _layer_spec(p["conv_b"].shape, 2),
        _layer_spec(p["ws_all"].shape, 2),
        _layer_spec(p["b_full"].shape, 2),
        pl.BlockSpec((1, D_MODEL), lambda d, b: (0, 0)),
    ]
    cache_block = pl.BlockSpec((1, 1, seq * A_HEADS, A_V_DIM), lambda d, b: (b, d, 0, 0))
    out_specs = [pl.BlockSpec(memory_space=pl.ANY), cache_block, cache_block]
    out_shape = [
        jax.ShapeDtypeStruct(x.shape, F32),
        jax.ShapeDtypeStruct((batch, DEPTH, seq * A_HEADS, A_V_DIM), F32),
        jax.ShapeDtypeStruct((batch, DEPTH, seq * A_HEADS, A_V_DIM), F32),
    ]
    scratch = [
        pltpu.VMEM((2, seq, D_MODEL), F32),
        pltpu.VMEM((2, seq, D_MODEL), F32),
        pltpu.SemaphoreType.DMA((2,)),
        pltpu.SemaphoreType.DMA((2,)),
        pltpu.VMEM((seq, D_MODEL), BF16),
    ]
    return pl.pallas_call(
        _ctx_kernel,
        grid=(DEPTH, batch),
        in_specs=in_specs,
        out_specs=out_specs,
        out_shape=out_shape,
        scratch_shapes=scratch,
        compiler_params=pltpu.CompilerParams(
            dimension_semantics=("arbitrary", "arbitrary"), vmem_limit_bytes=VMEM_LIMIT_BYTES),
        name="ctx",
    )(x, mod4, p["norm_mix"], p["norm_mlp"], p["w_in"], p["w_out"], p["w_up"], p["w_down"],
      p["lamv"], p["lamc"], p["subln"], p["conv_w"], p["conv_b"], p["ws_all"], p["b_full"],
      p["norm_final"])


def _rope(xh, cos, sin_signed):
    lane = lax.broadcasted_iota(jnp.int32, (1, A_V_DIM), 1)
    half = ROPE_AXIS_DIM // 2
    low = (lane & (ROPE_AXIS_DIM - 1)) < half
    partner = jnp.where(low, pltpu.roll(xh, A_V_DIM - half, 1), pltpu.roll(xh, half, 1))
    return xh * cos + partner * sin_signed


def _lat_kernel(x_ref, ck_ref, cv_ref, mod_ref, cos_ref, sin_ref, nmix_ref, nmlp_ref,
                w_in_ref, w_out_ref, w_up_ref, w_down_ref, lamv_ref, lamc_ref, subln_ref,
                cw_ref, cb_ref, ws_ref, bfull_ref, nfin_ref,
                y_ref, xs, q_s, k_s, v_s, gb_s, z_s, cm_s, mix_ref):
    d = pl.program_id(0)
    b = pl.program_id(1)
    ph = pl.program_id(2)
    t = pl.program_id(3)
    past = k_s.shape[0] - q_s.shape[0]
    n_tok = q_s.shape[0]
    t0 = pl.multiple_of(t * ROW_TILE, ROW_TILE)
    mod = _mod_parts(mod_ref[0, 0])
    x_rows = (b, pl.ds(t0, ROW_TILE))

    @pl.when(ph == 0)
    def _project():
        @pl.when(d == 0)
        def _():
            xs[x_rows] = x_ref[0]

        @pl.when(t == 0)
        def _():
            for hh in range(A_HEADS):
                rows = pl.ds(hh, past, stride=A_HEADS)
                k_s[0:past, hh * A_V_DIM:(hh + 1) * A_V_DIM] = ck_ref[0, 0, rows, :].astype(BF16)
                v_s[0:past, hh * A_V_DIM:(hh + 1) * A_V_DIM] = cv_ref[0, 0, rows, :].astype(BF16)
            z_s[0:CONV_PAD, :] = jnp.zeros((CONV_PAD, B_WIDTH), F32)
            z_s[CONV_PAD + n_tok:, :] = jnp.zeros((CONV_PAD, B_WIDTH), F32)

        sa, ca = mod[0], mod[1]
        h = (_rmsnorm(xs[x_rows], nmix_ref[0]) * (1.0 + ca) + sa).astype(BF16)
        qkv = _dot(h, w_in_ref[0, :, OFF_Q:OFF_GB])
        cos = cos_ref[...]
        sin = sin_ref[...]
        for hh in range(A_HEADS):
            lo = hh * A_V_DIM
            qh = _rope(qkv[:, OFF_Q + lo:OFF_Q + lo + A_V_DIM], cos, sin) * (A_QK_DIM ** -0.5)
            kh = _rope(qkv[:, OFF_K + lo:OFF_K + lo + A_V_DIM], cos, sin)
            q_s[pl.ds(t0, ROW_TILE), lo:lo + A_V_DIM] = qh
            k_s[pl.ds(past + t0, ROW_TILE), lo:lo + A_V_DIM] = kh.astype(BF16)
        v_s[pl.ds(past + t0, ROW_TILE), :] = qkv[:, OFF_V:OFF_GB].astype(BF16)

        rest = _dot(h, w_in_ref[0, :, OFF_GB:IN_WIDTH])
        r0 = OFF_GB
        gb_s[pl.ds(t0, ROW_TILE), :] = rest[:, OFF_GB - r0:OFF_GC - r0]
        z_s[pl.ds(CONV_PAD + t0, ROW_TILE), :] = (
            rest[:, OFF_GC - r0:OFF_XIN - r0] * rest[:, OFF_XIN - r0:OFF_U - r0])
        cm_s[pl.ds(t0, ROW_TILE), :] = _chunk_mlp(
            rest[:, OFF_U - r0:OFF_VS - r0], rest[:, OFF_VS - r0:], ws_ref[0], bfull_ref[0])

    @pl.when(ph == 1)
    def _mix():
        lam_init = lamc_ref[0, :, 0:1]
        one_minus_lam_init = lamc_ref[0, :, 1:2]
        lam = _lambda(lamv_ref[0], lam_init)
        att = _diff_attention(
            q_s[pl.ds(t0, ROW_TILE), :],
            lambda hh: k_s[:, hh * A_V_DIM:(hh + 1) * A_V_DIM],
            lambda hh: v_s[:, hh * A_V_DIM:(hh + 1) * A_V_DIM],
            lam, subln_ref[0], one_minus_lam_init)
        mix_ref[:, 0:A_WIDTH] = att.astype(BF16)

        zwin = z_s[pl.ds(t0, ROW_TILE + 2 * CONV_PAD), :]
        zp = zwin[CONV_PAD - 1:CONV_PAD - 1 + ROW_TILE]
        z = zwin[CONV_PAD:CONV_PAD + ROW_TILE]
        zn = zwin[CONV_PAD + 1:CONV_PAD + 1 + ROW_TILE]
        conv = _conv_taps(zp, z, zn, gb_s[pl.ds(t0, ROW_TILE), :], cw_ref[0], cb_ref[0])
        mix_ref[:, A_WIDTH:A_WIDTH + B_WIDTH] = conv.astype(BF16)
        mix_ref[:, A_WIDTH + B_WIDTH:] = cm_s[pl.ds(t0, ROW_TILE), :].astype(BF16)

        x2 = _out_and_mlp(xs[x_rows], mix_ref, mod, nmlp_ref[0], w_out_ref, w_up_ref, w_down_ref)

        @pl.when(d < DEPTH - 1)
        def _():
            xs[x_rows] = x2

        @pl.when(d == DEPTH - 1)
        def _():
            y_ref[0] = _rmsnorm(x2, nfin_ref[...])


def _lat_call(x, cache_k, cache_v, mod4, cos, sin, p):
    batch, n_tok, _ = x.shape
    past = cache_k.shape[2] // A_HEADS
    n_tiles = n_tok // ROW_TILE
    single = dict(pipeline_mode=pl.Buffered(1))
    last = DEPTH - 1

    def x_map(d, b, ph, t):
        first = d == 0
        return (jnp.where(first, b, batch - 1), jnp.where(first, t, n_tiles - 1), 0)

    def y_map(d, b, ph, t):
        return (jnp.where(d == last, b, 0), jnp.where(d == last, t * ph, 0), 0)

    cache_block = pl.BlockSpec((1, 1, past * A_HEADS, A_V_DIM), lambda d, b, ph, t: (b, d, 0, 0),
                               **single)
    in_specs = [
        pl.BlockSpec((1, ROW_TILE, D_MODEL), x_map),
        cache_block,
        cache_block,
        pl.BlockSpec((1, 1, 1, 6 * D_MODEL), lambda d, b, ph, t: (d, 1 + b, 0, 0)),
        pl.BlockSpec((ROW_TILE, A_V_DIM), lambda d, b, ph, t: (t, 0)),
        pl.BlockSpec((ROW_TILE, A_V_DIM), lambda d, b, ph, t: (t, 0)),
        _layer_spec(p["norm_mix"].shape, 4),
        _layer_spec(p["norm_mlp"].shape, 4),
        _layer_spec(p["w_in"].shape, 4, **single),
        _layer_spec(p["w_out"].shape, 4, **single),
        _layer_spec(p["w_up"].shape, 4, **single),
        _layer_spec(p["w_down"].shape, 4, **single),
        _layer_spec(p["lamv"].shape, 4),
        _layer_spec(p["lamc"].shape, 4),
        _layer_spec(p["subln"].shape, 4),
        _layer_spec(p["conv_w"].shape, 4),
        _layer_spec(p["conv_b"].shape, 4),
        _layer_spec(p["ws_all"].shape, 4),
        _layer_spec(p["b_full"].shape, 4),
        pl.BlockSpec((1, D_MODEL), lambda d, b, ph, t: (0, 0)),
    ]
    out_specs = pl.BlockSpec((1, ROW_TILE, D_MODEL), y_map)
    scratch = [
        pltpu.VMEM((batch, n_tok, D_MODEL), F32),
        pltpu.VMEM((n_tok, A_WIDTH), F32),
        pltpu.VMEM((past + n_tok, A_WIDTH), BF16),
        pltpu.VMEM((past + n_tok, A_WIDTH), BF16),
        pltpu.VMEM((n_tok, B_WIDTH), F32),
        pltpu.VMEM((n_tok + 2 * CONV_PAD, B_WIDTH), F32),
        pltpu.VMEM((n_tok, C_WIDTH), F32),
        pltpu.VMEM((ROW_TILE, D_MODEL), BF16),
    ]
    return pl.pallas_call(
        _lat_kernel,
        grid=(DEPTH, batch, 2, n_tiles),
        in_specs=in_specs,
        out_specs=out_specs,
        out_shape=jax.ShapeDtypeStruct(x.shape, F32),
        scratch_shapes=scratch,
        compiler_params=pltpu.CompilerParams(
            dimension_semantics=("arbitrary",) * 4, vmem_limit_bytes=VMEM_LIMIT_BYTES),
        name="lat",
    )(x, cache_k, cache_v, mod4, cos, sin, p["norm_mix"], p["norm_mlp"], p["w_in"], p["w_out"],
      p["w_up"], p["w_down"], p["lamv"], p["lamc"], p["subln"], p["conv_w"], p["conv_b"],
      p["ws_all"], p["b_full"], p["norm_final"])


def _rope_tables(n_tokens):
    rows = n_tokens // GRID_W
    row = jnp.repeat(jnp.arange(rows, dtype=F32), GRID_W)
    col = jnp.tile(jnp.arange(GRID_W, dtype=F32), rows)
    inv = ROPE_BASE ** (-jnp.arange(0, ROPE_AXIS_DIM, 2, dtype=F32) / ROPE_AXIS_DIM)
    ang_r = row[:, None] * inv
    ang_c = col[:, None] * inv
    cos = jnp.concatenate([jnp.cos(ang_r)] * 2 + [jnp.cos(ang_c)] * 2, axis=-1)
    sin = jnp.concatenate([-jnp.sin(ang_r), jnp.sin(ang_r), -jnp.sin(ang_c), jnp.sin(ang_c)], axis=-1)
    reps = A_V_DIM // A_QK_DIM
    return jnp.tile(cos, (1, reps)), jnp.tile(sin, (1, reps))


def kernel(x_prompt, x_sample, cache_k, cache_v, c, c_ctx, w_mod, b_mod, norm_mix, norm_mlp, w_in,
           lam_q1, lam_k1, lam_q2, lam_k2, subln, conv_w, conv_b, w_s, b_s, w_out, w_up, w_down,
           norm_final):
    batch, seq, _ = x_prompt.shape
    dec_batch, dec_seq, _ = x_sample.shape
    past = cache_k.shape[2]
    assert 1 + dec_batch <= MOD_ROWS

    c_rows = jnp.zeros((MOD_ROWS, D_MODEL), F32).at[0].set(c_ctx).at[1:1 + dec_batch].set(c)
    mod = _mod_call(c_rows, w_mod, b_mod)
    mod4 = mod.reshape(DEPTH, MOD_ROWS, 1, 6 * D_MODEL)

    lam_inits = np.array([0.8 - 0.6 * math.exp(-0.3 * d) for d in range(DEPTH)], np.float64)
    lamc = np.zeros((DEPTH, 1, 128), np.float32)
    lamc[:, 0, 0] = lam_inits
    lamc[:, 0, 1] = 1.0 - lam_inits
    params = {
        "norm_mix": norm_mix.reshape(DEPTH, 1, D_MODEL),
        "norm_mlp": norm_mlp.reshape(DEPTH, 1, D_MODEL),
        "w_in": w_in.astype(BF16),
        "w_out": w_out.astype(BF16),
        "w_up": w_up.astype(BF16),
        "w_down": w_down.astype(BF16),
        "lamv": jnp.stack([lam_q1, lam_k1, lam_q2, lam_k2], axis=1),
        "lamc": jnp.asarray(lamc),
        "subln": subln.reshape(DEPTH, 1, A_V_DIM),
        "conv_w": conv_w,
        "conv_b": conv_b.reshape(DEPTH, 1, B_WIDTH),
        "ws_all": w_s.reshape(DEPTH, C_GROUPS * CHUNK, CHUNK).astype(BF16),
        "b_full": jnp.repeat(jnp.swapaxes(b_s, 1, 2), C_GROUP_DIM, axis=2),
        "norm_final": norm_final.reshape(1, D_MODEL),
    }

    y_prompt, new_k, new_v = _ctx_call(x_prompt, mod4, params)

    cos, sin = _rope_tables(dec_seq)
    y_sample = _lat_call(
        x_sample,
        cache_k.reshape(dec_batch, DEPTH, past * A_HEADS, A_V_DIM),
        cache_v.reshape(dec_batch, DEPTH, past * A_HEADS, A_V_DIM),
        mod4, cos, sin, params)

    return (y_prompt, y_sample,
            new_k.reshape(batch, DEPTH, seq, A_HEADS, 2 * A_QK_DIM),
            new_v.reshape(batch, DEPTH, seq, A_HEADS, A_V_DIM))
```
